```python
import math
import jax, jax.numpy as jnp
from jax import lax
import numpy as np

D_MODEL = 1024
BATCH = 8
SEQ = 8192
DEPTH = 1

MIX_WIDTH = D_MODEL
DN_HEADS = 4
DN_HEAD_DIM = MIX_WIDTH // 2 // DN_HEADS
DN_WIDTH = DN_HEADS * DN_HEAD_DIM
CONV_WIDTH = 4
CHUNK = 64
POOL_WINDOWS = (2, 4, 8, 16)
POOL_GROUPS = len(POOL_WINDOWS)
POOL_WIDTH = MIX_WIDTH - DN_WIDTH
POOL_GROUP_DIM = POOL_WIDTH // POOL_GROUPS
IN_WIDTH = 4 * DN_WIDTH + 2 * DN_HEADS + POOL_WIDTH
PEER_HEADS = 8
N_KEYS = 128
N_EXPERTS = N_KEYS * N_KEYS
PEER_QUERY_DIM = 256
PEER_HALF = PEER_QUERY_DIM // 2
PEER_TOPK = 16
PEER_TOKEN_BLOCK = 128
EPS = 1e-6

kernel_name = "hymba_deltanet_pool_peer_block"


def _rmsnorm(x, w):
    xf = x.astype(jnp.float32)
    y = xf * lax.rsqrt(jnp.mean(xf * xf, axis=-1, keepdims=True) + EPS)
    return (y * w.astype(jnp.float32)).astype(x.dtype)


def _l2norm(x):
    return x * lax.rsqrt(jnp.sum(x * x, axis=-1, keepdims=True) + EPS)


def _causal_conv(x, w):
    c = x.shape[-1]
    return lax.conv_general_dilated(
        x, w[:, None, :].astype(x.dtype), window_strides=(1,),
        padding=((CONV_WIDTH - 1, 0),),
        dimension_numbers=("NWC", "WIO", "NWC"), feature_group_count=c)


def _chunk_gated_delta_rule(q, k, v, g, beta):
    b, s, h, dk = q.shape
    dv = v.shape[-1]
    n = s // CHUNK
    q = q * (dk ** -0.5)

    def to_chunks(t):
        return t.reshape(b, n, CHUNK, h, -1).transpose(0, 3, 1, 2, 4)

    q, k, v = to_chunks(q), to_chunks(k), to_chunks(v)
    g = g.reshape(b, n, CHUNK, h).transpose(0, 3, 1, 2)
    beta = beta.reshape(b, n, CHUNK, h).transpose(0, 3, 1, 2)
    g = jnp.cumsum(g, axis=-1)
    k_beta = k * beta[..., None]
    v_beta = v * beta[..., None]
    causal = jnp.tril(jnp.ones((CHUNK, CHUNK), dtype=bool))
    strict = jnp.tril(jnp.ones((CHUNK, CHUNK), dtype=bool), -1)
    decay = jnp.exp(jnp.where(causal, g[..., :, None] - g[..., None, :], -jnp.inf))
    a_mat = jnp.where(strict, jnp.einsum("bhnid,bhnjd->bhnij", k_beta, k) * decay, 0.0)
    eye = jnp.eye(CHUNK, dtype=q.dtype)
    t_mat = lax.linalg.triangular_solve(eye + a_mat, jnp.broadcast_to(eye, a_mat.shape),
                                        left_side=True, lower=True)
    w_vals = jnp.einsum("bhnij,bhnjd->bhnid", t_mat, v_beta)
    k_cumdecay = jnp.einsum("bhnij,bhnjd->bhnid", t_mat, k_beta * jnp.exp(g)[..., None])
    attn_intra = jnp.where(causal, jnp.einsum("bhnid,bhnjd->bhnij", q, k) * decay, 0.0)
    g_last = g[..., -1]
    k_decay = k * jnp.exp(g_last[..., None] - g)[..., None]
    q_decay = q * jnp.exp(g)[..., None]
    xs = (jnp.moveaxis(q_decay, 2, 0), jnp.moveaxis(k_cumdecay, 2, 0), jnp.moveaxis(w_vals, 2, 0),
          jnp.moveaxis(attn_intra, 2, 0), jnp.moveaxis(k_decay, 2, 0), jnp.moveaxis(g_last, 2, 0))

    def step(state, inp):
        qd, kcd, wv, attn_c, kd, gl = inp
        v_new = wv - jnp.einsum("bhcd,bhde->bhce", kcd, state)
        o = jnp.einsum("bhcd,bhde->bhce", qd, state) + jnp.einsum("bhij,bhje->bhie", attn_c, v_new)
        state = state * jnp.exp(gl)[..., None, None] + jnp.einsum("bhcd,bhce->bhde", kd, v_new)
        return state, o

    state0 = jnp.zeros((b, h, dk, dv), dtype=q.dtype)
    _, o = lax.scan(step, state0, xs)
    return o.transpose(1, 0, 3, 2, 4).reshape(b, s, h, dv)


def _multiscale_pool(p, w_pool, pool_scale):
    b, s, _ = p.shape
    pf = p.astype(jnp.float32).reshape(b, s, POOL_GROUPS, POOL_GROUP_DIM)
    cs = jnp.cumsum(pf, axis=1)
    pos = jnp.arange(1, s + 1, dtype=jnp.int32)
    outs = []
    for gi, w in enumerate(POOL_WINDOWS):
        c = cs[:, :, gi]
        lower = jnp.pad(c, ((0, 0), (w, 0), (0, 0)))[:, :s]
        count = jnp.minimum(pos, w).astype(jnp.float32)[None, :, None]
        outs.append((c - lower) / count - pf[:, :, gi])
    pooled = jnp.stack(outs, axis=2)
    mixed = jnp.einsum("bsgc,gcd->bsgd", pooled, w_pool.astype(jnp.float32))
    return mixed.reshape(b, s, POOL_WIDTH) * pool_scale.astype(jnp.float32)


def _peer(xn, w_query, sub_keys, expert_down, expert_up):
    b, s, d = xn.shape
    blocks = xn.reshape(-1, PEER_TOKEN_BLOCK, d)
    sk = sub_keys.astype(jnp.float32)

    def block(xb):
        t = xb.shape[0]
        q = (xb @ w_query).astype(jnp.float32).reshape(t, PEER_HEADS, 2, PEER_HALF)
        scores = jnp.einsum("thpc,hpkc->thpk", q, sk)
        s1, i1 = lax.top_k(scores[:, :, 0], PEER_TOPK)
        s2, i2 = lax.top_k(scores[:, :, 1], PEER_TOPK)
        cand = (s1[..., :, None] + s2[..., None, :]).reshape(t, PEER_HEADS, PEER_TOPK * PEER_TOPK)
        cand_idx = (i1[..., :, None] * N_KEYS + i2[..., None, :]).reshape(t, PEER_HEADS, PEER_TOPK * PEER_TOPK)
        top_s, top_pos = lax.top_k(cand, PEER_TOPK)
        idx = jnp.take_along_axis(cand_idx, top_pos, axis=-1)
        gate = jax.nn.softmax(top_s, axis=-1)
        u = expert_down[idx]
        v = expert_up[idx]
        act = jax.nn.gelu(jnp.einsum("td,thkd->thk", xb, u).astype(jnp.float32), approximate=False)
        return jnp.einsum("thk,thkd->td", (gate * act).astype(v.dtype), v)

    return lax.map(block, blocks).reshape(b, s, d).astype(xn.dtype)


def setup_inputs(seed: int = 0) -> dict:
    key = jax.random.key(seed)
    ks = jax.random.split(key, 16)
    f32 = jnp.float32
    L = DEPTH

    def gain(k, shape):
        return 1.0 + 0.02 * jax.random.normal(k, shape, f32)

    x = jax.random.normal(ks[0], (BATCH, SEQ, D_MODEL), f32)
    norm_mix_w = gain(ks[1], (L, D_MODEL))
    w_in = jax.random.normal(ks[2], (L, D_MODEL, IN_WIDTH), f32) * D_MODEL ** -0.5
    conv_w = jax.random.normal(ks[3], (L, CONV_WIDTH, 3 * DN_WIDTH), f32) * CONV_WIDTH ** -0.5
    a_log = jnp.log(jax.random.uniform(ks[4], (L, DN_HEADS), f32, 1.0, 16.0))
    dt = jnp.exp(jax.random.uniform(ks[5], (L, DN_HEADS), f32, math.log(1e-3), math.log(1e-1)))
    dt_bias = dt + jnp.log(-jnp.expm1(-dt))
    dn_norm_w = gain(ks[6], (L, DN_HEAD_DIM))
    w_pool = jax.random.normal(ks[7], (L, POOL_GROUPS, POOL_GROUP_DIM, POOL_GROUP_DIM), f32) * POOL_GROUP_DIM ** -0.5
    pool_scale = gain(ks[8], (L, POOL_WIDTH))
    w_out = jax.random.normal(ks[9], (L, MIX_WIDTH, D_MODEL), f32) * MIX_WIDTH ** -0.5
    norm_ffn_w = gain(ks[10], (L, D_MODEL))
    w_query = jax.random.normal(ks[11], (L, D_MODEL, PEER_HEADS * PEER_QUERY_DIM), f32) * D_MODEL ** -0.5
    sub_keys = jax.random.normal(ks[12], (L, PEER_HEADS, 2, N_KEYS, PEER_HALF), f32) * PEER_HALF ** -0.5
    expert_down = jax.random.normal(ks[13], (L, N_EXPERTS, D_MODEL), f32) * D_MODEL ** -0.5
    expert_up = jax.random.normal(ks[14], (L, N_EXPERTS, D_MODEL), f32) * PEER_HEADS ** -0.5
    norm_final_w = gain(ks[15], (D_MODEL,))
    return {"x": x, "norm_mix_w": norm_mix_w, "w_in": w_in, "conv_w": conv_w, "a_log": a_log,
            "dt_bias": dt_bias, "dn_norm_w": dn_norm_w, "w_pool": w_pool, "pool_scale": pool_scale,
            "w_out": w_out, "norm_ffn_w": norm_ffn_w, "w_query": w_query, "sub_keys": sub_keys,
            "expert_down": expert_down, "expert_up": expert_up, "norm_final_w": norm_final_w}


def reference(x, norm_mix_w, w_in, conv_w, a_log, dt_bias, dn_norm_w, w_pool, pool_scale,
              w_out, norm_ffn_w, w_query, sub_keys, expert_down, expert_up, norm_final_w):
    b, s, _ = x.shape
    for l in range(DEPTH):
        h = _rmsnorm(x, norm_mix_w[l])
        proj = h @ w_in[l]
        qkv = proj[..., :3 * DN_WIDTH]
        z = proj[..., 3 * DN_WIDTH:4 * DN_WIDTH]
        b_gate = proj[..., 4 * DN_WIDTH:4 * DN_WIDTH + DN_HEADS]
        a_dec = proj[..., 4 * DN_WIDTH + DN_HEADS:4 * DN_WIDTH + 2 * DN_HEADS]
        p = proj[..., 4 * DN_WIDTH + 2 * DN_HEADS:]
        qkv = jax.nn.silu(_causal_conv(qkv, conv_w[l])).astype(jnp.float32)
        q = _l2norm(qkv[..., :DN_WIDTH].reshape(b, s, DN_HEADS, DN_HEAD_DIM))
        k = _l2norm(qkv[..., DN_WIDTH:2 * DN_WIDTH].reshape(b, s, DN_HEADS, DN_HEAD_DIM))
        v = qkv[..., 2 * DN_WIDTH:].reshape(b, s, DN_HEADS, DN_HEAD_DIM)
        beta = jax.nn.sigmoid(b_gate.astype(jnp.float32))
        g = -jnp.exp(a_log[l].astype(jnp.float32)) * jax.nn.softplus(
            a_dec.astype(jnp.float32) + dt_bias[l].astype(jnp.float32))
        o = _chunk_gated_delta_rule(q, k, v, g, beta)
        o = _rmsnorm(o, dn_norm_w[l]) * jax.nn.silu(
            z.astype(jnp.float32).reshape(b, s, DN_HEADS, DN_HEAD_DIM))
        o = o.reshape(b, s, DN_WIDTH).astype(x.dtype)
        pool_out = _multiscale_pool(p, w_pool[l], pool_scale[l]).astype(x.dtype)
        x = x + jnp.concatenate([o, pool_out], axis=-1) @ w_out[l]
        x = x + _peer(_rmsnorm(x, norm_ffn_w[l]), w_query[l], sub_keys[l], expert_down[l], expert_up[l])
    return _rmsnorm(x, norm_final_w)
```

```python
import functools
import math

import jax
import jax.numpy as jnp
from jax import lax
from jax.experimental import pallas as pl
from jax.experimental.pallas import tpu as pltpu

DN_HEADS = 4
DN_HEAD_DIM = 128
DN_WIDTH = DN_HEADS * DN_HEAD_DIM
CONV_WIDTH = 4
CHUNK = 64
POOL_WINDOWS = (2, 4, 8, 16)
POOL_GROUP_DIM = 128
POOL_WIDTH = len(POOL_WINDOWS) * POOL_GROUP_DIM
PEER_HEADS = 8
N_KEYS = 128
PEER_HALF = 128
PEER_TOPK = 16
EPS = 1e-6

LANES = 128
SUBLANES = 8
VMEM_LIMIT_BYTES = 56 * 1024 * 1024

F32 = jnp.float32
BF16 = jnp.bfloat16
NEG_INF = float("-inf")
HIGHEST = lax.Precision.HIGHEST


def _dot(a, b):
    return jnp.dot(a, b, preferred_element_type=F32)


def _dot_nt(a, b):
    return lax.dot_general(a, b, (((1,), (1,)), ((), ())), preferred_element_type=F32)


def _dot_tn(a, b):
    return lax.dot_general(a, b, (((0,), (0,)), ((), ())), preferred_element_type=F32)


def _dot_hi(a, b):
    return jnp.dot(a, b, preferred_element_type=F32, precision=HIGHEST)


def _silu(x):
    return x * jax.nn.sigmoid(x)


def _gelu_exact(x):
    return 0.5 * x * (1.0 + lax.erf(x * math.sqrt(0.5)))


def _rms(x, w):
    return x * lax.rsqrt(jnp.mean(x * x, axis=-1, keepdims=True) + EPS) * w


def _in_proj_kernel(x_ref, nw_ref, wmain_ref, wgate_ref, wgate_t_ref, convw_ref,
                    alog_row_ref, dtb_row_ref, alog_col_ref, dtb_col_ref,
                    q_ref, k_ref, v_ref, z_ref, p_ref, gcol_ref, grow_ref, tail_ref):
    ts = x_ref.shape[0]
    n_chunks = ts // CHUNK

    @pl.when(pl.program_id(1) == 0)
    def _():
        tail_ref[...] = jnp.zeros_like(tail_ref)

    h = _rms(x_ref[...], nw_ref[...]).astype(BF16)
    proj = _dot(h, wmain_ref[...])
    qkv_raw = proj[:, :3 * DN_WIDTH]
    z_ref[...] = proj[:, 3 * DN_WIDTH:4 * DN_WIDTH]
    p_ref[...] = proj[:, 4 * DN_WIDTH:]

    ext = jnp.concatenate([tail_ref[...], qkv_raw], axis=0)
    convw = convw_ref[...]
    acc = None
    for j in range(CONV_WIDTH):
        off = SUBLANES - (CONV_WIDTH - 1) + j
        term = ext[off:off + ts, :] * convw[j:j + 1, :]
        acc = term if acc is None else acc + term
    tail_ref[...] = qkv_raw[ts - SUBLANES:, :]
    qkv = _silu(acc)
    for hd in range(DN_HEADS):
        sl = slice(hd * DN_HEAD_DIM, (hd + 1) * DN_HEAD_DIM)
        qh = qkv[:, sl]
        qn = qh * lax.rsqrt(jnp.sum(qh * qh, axis=-1, keepdims=True) + EPS)
        q_ref[:, sl] = qn * (DN_HEAD_DIM ** -0.5)
        kh = qkv[:, DN_WIDTH + hd * DN_HEAD_DIM:DN_WIDTH + (hd + 1) * DN_HEAD_DIM]
        k_ref[:, sl] = kh * lax.rsqrt(jnp.sum(kh * kh, axis=-1, keepdims=True) + EPS)
    v_ref[...] = qkv[:, 2 * DN_WIDTH:]

    gate = _dot(h, wgate_ref[...])
    beta_c = jax.nn.sigmoid(gate)
    g_c = -jnp.exp(alog_row_ref[...]) * jax.nn.softplus(gate + dtb_row_ref[...])
    r = lax.broadcasted_iota(jnp.int32, (ts, ts), 0)
    c = lax.broadcasted_iota(jnp.int32, (ts, ts), 1)
    ltri = jnp.where((r // CHUNK == c // CHUNK) & (c <= r), 1.0, 0.0).astype(F32)
    gc_c = _dot_hi(ltri, g_c)
    lane = lax.broadcasted_iota(jnp.int32, gate.shape, 1)
    gcol_ref[...] = jnp.where(lane < DN_HEADS, beta_c, gc_c)

    rr = lax.broadcasted_iota(jnp.int32, (CHUNK, CHUNK), 0)
    cc = lax.broadcasted_iota(jnp.int32, (CHUNK, CHUNK), 1)
    utri = jnp.where(rr <= cc, 1.0, 0.0).astype(F32)
    row = lax.broadcasted_iota(jnp.int32, (2 * DN_HEADS, CHUNK), 0)
    for ci in range(n_chunks):
        gate_t = _dot_nt(wgate_t_ref[...], h[ci * CHUNK:(ci + 1) * CHUNK, :])
        beta_r = jax.nn.sigmoid(gate_t)
        g_r = -jnp.exp(alog_col_ref[...]) * jax.nn.softplus(gate_t + dtb_col_ref[...])
        gc_r = _dot_hi(g_r, utri)
        grow_ref[ci] = jnp.where(row < DN_HEADS, beta_r, gc_r)


def _in_proj(x, nw, wmain, wgate, wgate_t, convw, alog_row, dtb_row, alog_col, dtb_col, ts):
    b, s, d = x.shape
    nw_main = wmain.shape[1]
    grid = (b, s // ts)
    tok = lambda w: pl.BlockSpec((None, ts, w), lambda i, j: (i, j, 0))
    full = lambda a: pl.BlockSpec(a.shape, lambda i, j: (0,) * a.ndim)
    out_shapes = (
        jax.ShapeDtypeStruct((b, s, DN_WIDTH), F32),
        jax.ShapeDtypeStruct((b, s, DN_WIDTH), F32),
        jax.ShapeDtypeStruct((b, s, DN_WIDTH), F32),
        jax.ShapeDtypeStruct((b, s, DN_WIDTH), F32),
        jax.ShapeDtypeStruct((b, s, POOL_WIDTH), F32),
        jax.ShapeDtypeStruct((b, s, LANES), F32),
        jax.ShapeDtypeStruct((b, s // CHUNK, 2 * DN_HEADS, CHUNK), F32),
    )
    out_specs = (
        tok(DN_WIDTH), tok(DN_WIDTH), tok(DN_WIDTH), tok(DN_WIDTH), tok(POOL_WIDTH), tok(LANES),
        pl.BlockSpec((None, ts // CHUNK, 2 * DN_HEADS, CHUNK), lambda i, j: (i, j, 0, 0)),
    )
    return pl.pallas_call(
        _in_proj_kernel,
        grid=grid,
        in_specs=[tok(d), full(nw), full(wmain), full(wgate), full(wgate_t), full(convw),
                  full(alog_row), full(dtb_row), full(alog_col), full(dtb_col)],
        out_specs=out_specs,
        out_shape=out_shapes,
        scratch_shapes=[pltpu.VMEM((SUBLANES, 3 * DN_WIDTH), F32)],
        compiler_params=pltpu.CompilerParams(
            dimension_semantics=("arbitrary", "arbitrary"), vmem_limit_bytes=VMEM_LIMIT_BYTES),
        name="in_proj",
    )(x, nw, wmain, wgate, wgate_t, convw, alog_row, dtb_row, alog_col, dtb_col)


def _inv_unit_lower(a):
    n = a.shape[0]
    r = lax.broadcasted_iota(jnp.int32, (n, n), 0)
    c = lax.broadcasted_iota(jnp.int32, (n, n), 1)
    same16 = (r // 16) == (c // 16)
    same32 = (r // 32) == (c // 32)
    eye = jnp.where(r == c, 1.0, 0.0).astype(F32)
    b1 = jnp.where(same16, -a, 0.0)
    m = eye + b1
    b2 = _dot_hi(b1, b1)
    m = m + _dot_hi(m, b2)
    b4 = _dot_hi(b2, b2)
    m = m + _dot_hi(m, b4)
    b8 = _dot_hi(b4, b4)
    m = m + _dot_hi(m, b8)
    q32 = jnp.where(same32 & jnp.logical_not(same16), a, 0.0)
    m = m - _dot_hi(m, _dot_hi(q32, m))
    q64 = jnp.where(same32, 0.0, a)
    m = m - _dot_hi(m, _dot_hi(q64, m))
    return m


def _deltanet_kernel(q_ref, k_ref, v_ref, z_ref, gcol_ref, grow_ref, dnw_ref, o_ref, state_ref):
    tb = q_ref.shape[0]
    n_chunks = tb // CHUNK

    @pl.when(pl.program_id(1) == 0)
    def _():
        state_ref[...] = jnp.zeros_like(state_ref)

    r = lax.broadcasted_iota(jnp.int32, (CHUNK, CHUNK), 0)
    c = lax.broadcasted_iota(jnp.int32, (CHUNK, CHUNK), 1)
    causal = c <= r
    strict = c < r
    dnw = dnw_ref[...]

    def chunk_body(ci, carry):
        rows = pl.ds(pl.multiple_of(ci * CHUNK, CHUNK), CHUNK)
        gcol = gcol_ref[rows, :]
        grow = grow_ref[ci]
        for hd in range(DN_HEADS):
            sl = slice(hd * DN_HEAD_DIM, (hd + 1) * DN_HEAD_DIM)
            q = q_ref[rows, sl]
            k = k_ref[rows, sl]
            v = v_ref[rows, sl]
            beta = gcol[:, hd:hd + 1]
            gc = gcol[:, DN_HEADS + hd:DN_HEADS + hd + 1]
            gr = grow[DN_HEADS + hd:DN_HEADS + hd + 1, :]
            g_last = gr[:, CHUNK - 1:CHUNK]
            decay = jnp.exp(jnp.where(causal, gc - gr, NEG_INF))
            kb = k * beta
            k16 = k.astype(BF16)
            a_mat = jnp.where(strict, _dot_nt(kb.astype(BF16), k16) * decay, 0.0)
            t_mat = _inv_unit_lower(a_mat).astype(BF16)
            w_vals = _dot(t_mat, (v * beta).astype(BF16))
            e_gc = jnp.exp(gc)
            kcd = _dot(t_mat, (kb * e_gc).astype(BF16))
            q16 = q.astype(BF16)
            attn = jnp.where(causal, _dot_nt(q16, k16) * decay, 0.0)
            k_decay = k * jnp.exp(g_last - gc)
            q_decay = q * e_gc
            state = state_ref[hd]
            s16 = state.astype(BF16)
            v_new = w_vals - _dot(kcd.astype(BF16), s16)
            vn16 = v_new.astype(BF16)
            o = _dot(q_decay.astype(BF16), s16) + _dot(attn.astype(BF16), vn16)
            state_ref[hd] = state * jnp.exp(g_last) + _dot_tn(k_decay.astype(BF16), vn16)
            o = _rms(o, dnw) * _silu(z_ref[rows, sl])
            o_ref[rows, sl] = o.astype(o_ref.dtype)
        return carry

    lax.fori_loop(0, n_chunks, chunk_body, 0)


def _deltanet(q, k, v, z, gcol, grow, dnw, tb):
    b, s, _ = q.shape
    grid = (b, s // tb)
    tok = lambda w: pl.BlockSpec((None, tb, w), lambda i, j: (i, j, 0))
    return pl.pallas_call(
        _deltanet_kernel,
        grid=grid,
        in_specs=[tok(DN_WIDTH), tok(DN_WIDTH), tok(DN_WIDTH), tok(DN_WIDTH), tok(LANES),
                  pl.BlockSpec((None, tb // CHUNK, 2 * DN_HEADS, CHUNK), lambda i, j: (i, j, 0, 0)),
                  pl.BlockSpec(dnw.shape, lambda i, j: (0, 0))],
        out_specs=tok(DN_WIDTH),
        out_shape=jax.ShapeDtypeStruct((b, s, DN_WIDTH), BF16),
        scratch_shapes=[pltpu.VMEM((DN_HEADS, DN_HEAD_DIM, DN_HEAD_DIM), F32)],
        compiler_params=pltpu.CompilerParams(
            dimension_semantics=("arbitrary", "arbitrary"), vmem_limit_bytes=VMEM_LIMIT_BYTES),
        name="deltanet",
    )(q, k, v, z, gcol, grow, dnw)


POOL_HALO = 16


def _out_proj_kernel(x_ref, o_ref, p_ref, wpool_ref, pscale_ref, wout_ref, nfw_ref,
                     x1_ref, xnt_ref, tail_ref):
    ts = x_ref.shape[0]

    @pl.when(pl.program_id(1) == 0)
    def _():
        tail_ref[...] = jnp.zeros_like(tail_ref)

    p = p_ref[...]
    ext = jnp.concatenate([tail_ref[...], p], axis=0)
    tail_ref[...] = p[ts - POOL_HALO:, :]
    pos = pl.program_id(1) * ts + lax.broadcasted_iota(jnp.int32, (ts, 1), 0) + 1
    mixed = []
    for gi, w in enumerate(POOL_WINDOWS):
        sl = slice(gi * POOL_GROUP_DIM, (gi + 1) * POOL_GROUP_DIM)
        e = ext[:, sl]
        n = POOL_HALO + ts
        d = 1
        while d < w:
            e = e[d:, :] + e[:n - d, :]
            n -= d
            d *= 2
        win = e[POOL_HALO - (w - 1):POOL_HALO - (w - 1) + ts, :]
        count = jnp.minimum(pos, w).astype(F32)
        pooled = win / count - p[:, sl]
        mixed.append(_dot(pooled.astype(BF16), wpool_ref[gi]))
    pool_out = jnp.concatenate(mixed, axis=1) * pscale_ref[...]
    cat = jnp.concatenate([o_ref[...], pool_out.astype(BF16)], axis=1)
    x1 = x_ref[...] + _dot(cat, wout_ref[...])
    x1_ref[...] = x1
    xn = _rms(x1, nfw_ref[...])
    xnt_ref[...] = xn.T.astype(xnt_ref.dtype)


def _out_proj(x, o, p, wpool, pscale, wout, nfw, ts):
    b, s, d = x.shape
    grid = (b, s // ts)
    nblk = s // ts
    tok = lambda w: pl.BlockSpec((None, ts, w), lambda i, j: (i, j, 0))
    full = lambda a: pl.BlockSpec(a.shape, lambda i, j: (0,) * a.ndim)
    return pl.pallas_call(
        _out_proj_kernel,
        grid=grid,
        in_specs=[tok(d), tok(DN_WIDTH), tok(POOL_WIDTH), full(wpool), full(pscale), full(wout),
                  full(nfw)],
        out_specs=(tok(d), pl.BlockSpec((d, ts), lambda i, j: (0, i * nblk + j))),
        out_shape=(jax.ShapeDtypeStruct((b, s, d), F32),
                   jax.ShapeDtypeStruct((d, b * s), BF16)),
        scratch_shapes=[pltpu.VMEM((POOL_HALO, POOL_WIDTH), F32)],
        compiler_params=pltpu.CompilerParams(
            dimension_semantics=("arbitrary", "arbitrary"), vmem_limit_bytes=VMEM_LIMIT_BYTES),
        name="out_proj",
    )(x, o, p, wpool, pscale, wout, nfw)


SELECT_LANES = 128


def _top_values(s, n):
    out = []
    cur = s
    for _ in range(n):
        m = jnp.max(cur, axis=0, keepdims=True)
        out.append(m)
        cur = jnp.where(cur >= m, NEG_INF, cur)
    return out


def _candidate_blocks(t1, t2):
    s1 = jnp.concatenate(t1, axis=0)
    s2 = jnp.concatenate(t2, axis=0)
    row16 = lax.broadcasted_iota(jnp.int32, s1.shape, 0)
    row8 = row16[:SUBLANES]
    blocks = [t1[0] + s2]
    for a in (1, 2, 3):
        lim = PEER_TOPK // (a + 1)
        blocks.append(jnp.where(row8 < lim, t1[a] + s2[:SUBLANES], NEG_INF))
    blocks.append(jnp.where(row16 >= 4, s1 + t2[0], NEG_INF))
    blocks.append(jnp.where(row8 >= 4, s1[:SUBLANES] + t2[1], NEG_INF))
    blocks.append(jnp.where(row8 == 4, s1[:SUBLANES] + t2[2], NEG_INF))
    return blocks


def _peer_select_kernel(xnt_ref, wqt_ref, sk_ref, lim_ref, e1_ref, r2_ref, e2_ref, s_ref):
    tm = xnt_ref.shape[1]
    qt = _dot(wqt_ref[...], xnt_ref[...]).astype(BF16)
    for hp in range(2 * PEER_HEADS):
        s_ref[hp] = _dot(sk_ref[hp], qt[hp * PEER_HALF:(hp + 1) * PEER_HALF, :])

    n_lb = tm // SELECT_LANES

    def body(it, carry):
        hd = it // n_lb
        lb = it % n_lb
        lanes = pl.ds(pl.multiple_of(lb * SELECT_LANES, SELECT_LANES), SELECT_LANES)
        s1 = s_ref[2 * hd, :, lanes]
        s2 = s_ref[2 * hd + 1, :, lanes]
        t1 = _top_values(s1, PEER_TOPK)
        t2 = _top_values(s2, PEER_TOPK)
        blocks = _candidate_blocks(t1, t2)
        cur = blocks
        theta = None
        for _ in range(PEER_TOPK):
            m = None
            for blk in cur:
                bm = jnp.max(blk, axis=0, keepdims=True)
                m = bm if m is None else jnp.maximum(m, bm)
            theta = m
            cur = [jnp.where(blk >= m, NEG_INF, blk) for blk in cur]
        top = t1[0] + t2[0]
        zsum = None
        for blk in blocks:
            part = jnp.sum(jnp.where(blk >= theta, jnp.exp(blk - top), 0.0), axis=0, keepdims=True)
            zsum = part if zsum is None else zsum + part
        e1 = jnp.exp(s1 - t1[0])
        e2 = jnp.exp(s2 - t2[0]) / zsum
        r2 = jnp.zeros_like(s2)
        lim = jnp.zeros_like(s1)
        for b in range(PEER_TOPK):
            r2 = r2 + jnp.where(t2[b] > s2, 1.0, 0.0)
            lim = lim + jnp.where(s1 + t2[b] >= theta, 1.0, 0.0)
        rows = pl.ds(pl.multiple_of(hd * N_KEYS, N_KEYS), N_KEYS)
        lim_ref[rows, lanes] = lim.astype(lim_ref.dtype)
        e1_ref[rows, lanes] = e1.astype(e1_ref.dtype)
        r2_ref[rows, lanes] = r2.astype(r2_ref.dtype)
        e2_ref[rows, lanes] = e2.astype(e2_ref.dtype)
        return carry

    lax.fori_loop(0, PEER_HEADS * n_lb, body, 0)


def _peer_select(xnt, wqt, sk, tm):
    d, t = xnt.shape
    grid = (t // tm,)
    full = lambda a: pl.BlockSpec(a.shape, lambda i: (0,) * a.ndim)
    desc = jax.ShapeDtypeStruct((PEER_HEADS * N_KEYS, t), BF16)
    desc_row = jax.ShapeDtypeStruct((PEER_HEADS * N_KEYS, t), F32)
    dspec = pl.BlockSpec((PEER_HEADS * N_KEYS, tm), lambda i: (0, i))
    return pl.pallas_call(
        _peer_select_kernel,
        grid=grid,
        in_specs=[pl.BlockSpec((d, tm), lambda i: (0, i)), full(wqt), full(sk)],
        out_specs=(dspec, dspec, dspec, dspec),
        out_shape=(desc_row, desc_row, desc, desc),
        scratch_shapes=[pltpu.VMEM((2 * PEER_HEADS, N_KEYS, tm), F32)],
        compiler_params=pltpu.CompilerParams(
            dimension_semantics=("arbitrary",), vmem_limit_bytes=VMEM_LIMIT_BYTES),
        name="peer_select",
    )(xnt, wqt, sk)


def _peer_dense_kernel(xnt_ref, ed_ref, eut_ref, lim_ref, e1_ref, r2_ref, e2_ref, x1_ref, fw_ref,
                       out_ref, acc_ref):
    j = pl.program_id(1)
    te = ed_ref.shape[0]
    n_i = te // N_KEYS

    @pl.when(j == 0)
    def _():
        acc_ref[...] = jnp.zeros_like(acc_ref)

    act_t = _dot(ed_ref[...], xnt_ref[...])
    h_parts = []
    for il in range(n_i):
        i_glob = j * n_i + il
        gate = None
        for hd in range(PEER_HEADS):
            hrows = slice(hd * N_KEYS, (hd + 1) * N_KEYS)
            lim_row = lim_ref[pl.ds(hd * N_KEYS + i_glob, 1), :].astype(BF16)
            e1_row = e1_ref[pl.ds(hd * N_KEYS + i_glob, 1), :].astype(BF16)
            term = jnp.where(r2_ref[hrows, :] < lim_row, e2_ref[hrows, :] * e1_row,
                             jnp.zeros((), BF16))
            gate = term if gate is None else gate + term
        a = act_t[il * N_KEYS:(il + 1) * N_KEYS, :]
        h_parts.append((_gelu_exact(a) * gate.astype(F32)).astype(BF16))
    h_t = jnp.concatenate(h_parts, axis=0)
    acc_ref[...] += _dot(eut_ref[...], h_t)

    @pl.when(j == pl.num_programs(1) - 1)
    def _():
        x2 = x1_ref[...] + acc_ref[...].T
        out_ref[...] = _rms(x2, fw_ref[...])


def _peer_dense(xnt, ed, eut, lim, e1, r2, e2, x1, fw, tm, te):
    d, t = xnt.shape
    n_exp = ed.shape[0]
    grid = (t // tm, n_exp // te)
    tokt = pl.BlockSpec((d, tm), lambda i, j: (0, i))
    dspec = pl.BlockSpec((PEER_HEADS * N_KEYS, tm), lambda i, j: (0, i))
    return pl.pallas_call(
        _peer_dense_kernel,
        grid=grid,
        in_specs=[tokt,
                  pl.BlockSpec((te, d), lambda i, j: (j, 0)),
                  pl.BlockSpec((d, te), lambda i, j: (0, j)),
                  dspec, dspec, dspec, dspec,
                  pl.BlockSpec((tm, d), lambda i, j: (i, 0)),
                  pl.BlockSpec(fw.shape, lambda i, j: (0, 0))],
        out_specs=pl.BlockSpec((tm, d), lambda i, j: (i, 0)),
        out_shape=jax.ShapeDtypeStruct((t, d), F32),
        scratch_shapes=[pltpu.VMEM((d, tm), F32)],
        compiler_params=pltpu.CompilerParams(
            dimension_semantics=("arbitrary", "arbitrary"), vmem_limit_bytes=VMEM_LIMIT_BYTES),
        name="peer_dense",
    )(xnt, ed, eut, lim, e1, r2, e2, x1, fw)


def _pick_tile(n, target):
    t = min(n, target)
    assert n % t == 0, (n, t)
    return t


def kernel(x, norm_mix_w, w_in, conv_w, a_log, dt_bias, dn_norm_w, w_pool, pool_scale, w_out,
           norm_ffn_w, w_query, sub_keys, expert_down, expert_up, norm_final_w):
    b, s, d = x.shape
    depth = w_in.shape[0]
    t = b * s
    ts = _pick_tile(s, 512)
    tm_sel = _pick_tile(t, 512)
    tm = _pick_tile(t, 512)
    te = 512
    h2 = 2 * DN_HEADS
    for l in range(depth):
        wi = w_in[l]
        c0 = 4 * DN_WIDTH
        wmain = jnp.concatenate([wi[:, :c0], wi[:, c0 + h2:]], axis=1).astype(BF16)
        wg = wi[:, c0:c0 + h2]
        wgate = jnp.pad(wg, ((0, 0), (0, LANES - h2))).astype(BF16)
        wgate_t = wg.T.astype(BF16)
        zeros_h = jnp.zeros((DN_HEADS,), F32)
        alog8 = jnp.concatenate([zeros_h, a_log[l].astype(F32)])
        dtb8 = jnp.concatenate([zeros_h, dt_bias[l].astype(F32)])
        alog_row = jnp.pad(alog8, (0, LANES - h2)).reshape(1, LANES)
        dtb_row = jnp.pad(dtb8, (0, LANES - h2)).reshape(1, LANES)
        q, k, v, z, p, gcol, grow = _in_proj(
            x, norm_mix_w[l].reshape(1, d), wmain, wgate, wgate_t, conv_w[l],
            alog_row, dtb_row, alog8.reshape(h2, 1), dtb8.reshape(h2, 1), ts)
        o = _deltanet(q, k, v, z, gcol, grow, dn_norm_w[l].reshape(1, DN_HEAD_DIM), ts)
        x1, xnt = _out_proj(x, o, p, w_pool[l].astype(BF16), pool_scale[l].reshape(1, POOL_WIDTH),
                            w_out[l].astype(BF16), norm_ffn_w[l].reshape(1, d), ts)
        wqt = w_query[l].T.astype(BF16)
        sk = sub_keys[l].reshape(2 * PEER_HEADS, N_KEYS, PEER_HALF).astype(BF16)
        lim, e1, r2, e2 = _peer_select(xnt, wqt, sk, tm_sel)
        ed = expert_down[l].astype(BF16)
        eut = expert_up[l].T.astype(BF16)
        last = l == depth - 1
        fw = norm_final_w.reshape(1, d) if last else None
        assert last, "peer_dense fuses the final norm; deeper stacks are not supported"
        x = _peer_dense(xnt, ed, eut, lim, e1, r2, e2, x1.reshape(t, d), fw, tm, te).reshape(b, s, d)
    return x
```
